```python
import math
import jax
import jax.numpy as jnp
from jax import lax
import numpy as np

D_MODEL = 1024
BATCH = 16
SEQ = 2048
DEPTH = 4

N_EVEN = (DEPTH + 1) // 2
N_ODD = DEPTH // 2
MIX_W = D_MODEL
A_WIDTH = MIX_W // 2
A_GROUPS = 8
A_GDIM = A_WIDTH // A_GROUPS
CHUNK = 128
B_WIDTH = MIX_W // 2
B_HEADS = 8
B_HDIM = B_WIDTH // B_HEADS
QBLOCK = 128
C_HEADS = 4
C_V = MIX_W // C_HEADS
C_QK = C_V // 2
C_CHUNK = 128
CONV_W = 4
D_FF = 4 * D_MODEL
EV_IN = 2 * A_WIDTH + 3 * B_WIDTH
OD_IN = 2 * C_HEADS * C_QK + 2 * C_HEADS * C_V + 2 * C_HEADS
EPS = 1e-6

kernel_name = 'hybrid_gmlp_stickbreak_mlstm_trunk'


def _rmsnorm(x, g):
    xf = x.astype(jnp.float32)
    ms = jnp.mean(xf * xf, axis=-1, keepdims=True)
    return (xf * lax.rsqrt(ms + EPS) * g.astype(jnp.float32)).astype(x.dtype)


def _layernorm(x, g, b):
    xf = x.astype(jnp.float32)
    mu = jnp.mean(xf, axis=-1, keepdims=True)
    var = jnp.mean(jnp.square(xf - mu), axis=-1, keepdims=True)
    y = (xf - mu) * lax.rsqrt(var + EPS) * g.astype(jnp.float32) + b.astype(jnp.float32)
    return y.astype(x.dtype)


def _stick_breaking(q, k, v):
    seq = q.shape[2]
    scale = q.shape[-1] ** -0.5
    outs = []
    for blk in range(seq // QBLOCK):
        t0 = blk * QBLOCK
        t1 = t0 + QBLOCK
        z = jnp.einsum('bhtd,bhsd->bhts', q[:, :, t0:t1], k[:, :, :t1]) * scale
        mask = jnp.arange(t1)[None, :] < jnp.arange(t0, t1)[:, None]
        log1m = jnp.where(mask, -jax.nn.softplus(z), 0.0)
        suffix = lax.cumsum(log1m, axis=3, reverse=True)
        a = jnp.where(mask, jnp.exp(z + suffix), 0.0)
        outs.append(jnp.einsum('bhts,bhsd->bhtd', a, v[:, :, :t1]))
    return jnp.concatenate(outs, axis=2)


def _even_mixer(h, w_in, w_out, ln_g, ln_b, sg_w, sg_b, qn_g, kn_g):
    bsz, seq, _ = h.shape
    nc = seq // CHUNK
    p = h @ w_in
    za = jax.nn.gelu(p[..., :2 * A_WIDTH])
    zb = p[..., 2 * A_WIDTH:]
    u = za[..., :A_WIDTH].reshape(bsz, nc, CHUNK, A_GROUPS, A_GDIM)
    vg = _layernorm(za[..., A_WIDTH:], ln_g, ln_b).reshape(bsz, nc, CHUNK, A_GROUPS, A_GDIM)
    w_causal = sg_w * jnp.tril(jnp.ones((CHUNK, CHUNK), sg_w.dtype))
    mixed = jnp.einsum('gts,bnsgc->bntgc', w_causal, vg) + sg_b.T[:, :, None]
    y_a = (u * mixed).reshape(bsz, seq, A_WIDTH)
    q = _rmsnorm(zb[..., :B_WIDTH].reshape(bsz, seq, B_HEADS, B_HDIM), qn_g)
    k = _rmsnorm(zb[..., B_WIDTH:2 * B_WIDTH].reshape(bsz, seq, B_HEADS, B_HDIM), kn_g)
    v = zb[..., 2 * B_WIDTH:].reshape(bsz, seq, B_HEADS, B_HDIM)
    to_bhsd = lambda t: t.astype(jnp.float32).transpose(0, 2, 1, 3)
    o = _stick_breaking(to_bhsd(q), to_bhsd(k), to_bhsd(v))
    y_b = o.transpose(0, 2, 1, 3).reshape(bsz, seq, B_WIDTH).astype(h.dtype)
    return jnp.concatenate([y_a, y_b], axis=-1) @ w_out


def _causal_conv(x, w, b):
    ch = x.shape[-1]
    y = lax.conv_general_dilated(x, w[:, None, :], window_strides=(1,),
                                 padding=[(CONV_W - 1, 0)],
                                 dimension_numbers=('NWC', 'WIO', 'NWC'),
                                 feature_group_count=ch)
    return y + b


def _mlstm(q, k, v, i_pre, f_pre):
    bsz, seq, nh, dqk = q.shape
    dv = v.shape[-1]
    nc = seq // C_CHUNK
    f32 = jnp.float32
    q = q.astype(f32)
    k = k.astype(f32) * (dqk ** -0.5)
    v = v.astype(f32)
    logf = jax.nn.log_sigmoid(f_pre.astype(f32))
    logi = i_pre.astype(f32)

    def chunk4(a):
        return a.reshape(bsz, nc, C_CHUNK, nh, a.shape[-1]).transpose(1, 0, 3, 2, 4)

    def chunk3(a):
        return a.reshape(bsz, nc, C_CHUNK, nh).transpose(1, 0, 3, 2)

    tril = jnp.tril(jnp.ones((C_CHUNK, C_CHUNK), bool))

    def step(carry, inp):
        c_st, n_st, m_st = carry
        qc, kc, vc, lf, li = inp
        bcum = jnp.cumsum(lf, axis=-1)
        dmat = jnp.where(tril, bcum[..., :, None] - bcum[..., None, :] + li[..., None, :], -jnp.inf)
        g = bcum + m_st[..., None]
        m_t = jnp.maximum(g, jnp.max(dmat, axis=-1))
        wts = jnp.exp(dmat - m_t[..., None])
        inter = jnp.exp(g - m_t)
        s = jnp.einsum('bhtd,bhsd->bhts', qc, kc) * wts
        num = jnp.einsum('bhts,bhsv->bhtv', s, vc) + inter[..., None] * jnp.einsum('bhtd,bhdv->bhtv', qc, c_st)
        den = jnp.sum(s, axis=-1) + inter * jnp.einsum('bhtd,bhd->bht', qc, n_st)
        h = num / jnp.maximum(jnp.abs(den), jnp.exp(-m_t))[..., None]
        b_last = bcum[..., -1]
        wk = b_last[..., None] - bcum + li
        m_new = jnp.maximum(b_last + m_st, jnp.max(wk, axis=-1))
        decay = jnp.exp(b_last + m_st - m_new)
        wkx = jnp.exp(wk - m_new[..., None])
        c_new = decay[..., None, None] * c_st + jnp.einsum('bhs,bhsd,bhsv->bhdv', wkx, kc, vc)
        n_new = decay[..., None] * n_st + jnp.einsum('bhs,bhsd->bhd', wkx, kc)
        return (c_new, n_new, m_new), h

    init = (jnp.zeros((bsz, nh, dqk, dv), f32), jnp.zeros((bsz, nh, dqk), f32), jnp.zeros((bsz, nh), f32))
    _, hs = lax.scan(step, init, (chunk4(q), chunk4(k), chunk4(v), chunk3(logf), chunk3(logi)))
    return hs.transpose(1, 0, 3, 2, 4).reshape(bsz, seq, nh, dv)


def _odd_mixer(h, w_in, conv_w, conv_b, i_b, f_b, on_g, w_out):
    bsz, seq, _ = h.shape
    p = h @ w_in
    n_qk = 2 * C_HEADS * C_QK
    n_v = C_HEADS * C_V
    qk = jax.nn.silu(_causal_conv(p[..., :n_qk], conv_w, conv_b))
    q = qk[..., :C_HEADS * C_QK].reshape(bsz, seq, C_HEADS, C_QK)
    k = qk[..., C_HEADS * C_QK:].reshape(bsz, seq, C_HEADS, C_QK)
    v = p[..., n_qk:n_qk + n_v].reshape(bsz, seq, C_HEADS, C_V)
    og = p[..., n_qk + n_v:n_qk + 2 * n_v].reshape(bsz, seq, C_HEADS, C_V)
    gates = p[..., n_qk + 2 * n_v:]
    i_pre = gates[..., :C_HEADS] + i_b
    f_pre = gates[..., C_HEADS:] + f_b
    hc = _mlstm(q, k, v, i_pre, f_pre)
    hc = _rmsnorm(hc, on_g.reshape(C_HEADS, C_V)).astype(h.dtype)
    y = (hc * jax.nn.sigmoid(og)).reshape(bsz, seq, MIX_W)
    return y @ w_out


def _sqrelu_mlp(h, w1, w2):
    return jnp.square(jax.nn.relu(h @ w1)) @ w2


def setup_inputs(seed: int = 0) -> dict:
    key = jax.random.key(seed)
    ks = jax.random.split(key, 24)
    nrm = lambda k, shape, s: jax.random.normal(k, shape, jnp.float32) * s
    gain = lambda k, shape: 1.0 + 0.05 * jax.random.normal(k, shape, jnp.float32)
    f_base = jnp.linspace(3.0, 6.0, C_HEADS, dtype=jnp.float32)
    return {
        'x': nrm(ks[0], (BATCH, SEQ, D_MODEL), 1.0),
        'mix_norm_g': gain(ks[1], (DEPTH, D_MODEL)),
        'mlp_norm_g': gain(ks[2], (DEPTH, D_MODEL)),
        'mlp_w1': nrm(ks[3], (DEPTH, D_MODEL, D_FF), D_MODEL ** -0.5),
        'mlp_w2': nrm(ks[4], (DEPTH, D_FF, D_MODEL), D_FF ** -0.5),
        'ev_w_in': nrm(ks[5], (N_EVEN, D_MODEL, EV_IN), D_MODEL ** -0.5),
        'ev_w_out': nrm(ks[6], (N_EVEN, MIX_W, D_MODEL), MIX_W ** -0.5),
        'sg_ln_g': gain(ks[7], (N_EVEN, A_WIDTH)),
        'sg_ln_b': nrm(ks[8], (N_EVEN, A_WIDTH), 0.02),
        'sg_w': nrm(ks[9], (N_EVEN, A_GROUPS, CHUNK, CHUNK), CHUNK ** -0.5),
        'sg_b': 1.0 + nrm(ks[10], (N_EVEN, A_GROUPS, CHUNK), 0.1),
        'sb_q_norm_g': gain(ks[11], (N_EVEN, B_HDIM)),
        'sb_k_norm_g': gain(ks[12], (N_EVEN, B_HDIM)),
        'od_w_in': nrm(ks[13], (N_ODD, D_MODEL, OD_IN), D_MODEL ** -0.5),
        'od_conv_w': nrm(ks[14], (N_ODD, CONV_W, 2 * C_HEADS * C_QK), CONV_W ** -0.5),
        'od_conv_b': nrm(ks[15], (N_ODD, 2 * C_HEADS * C_QK), 0.02),
        'od_i_b': nrm(ks[16], (N_ODD, C_HEADS), 0.1),
        'od_f_b': f_base[None, :] + nrm(ks[17], (N_ODD, C_HEADS), 0.1),
        'od_out_norm_g': gain(ks[18], (N_ODD, MIX_W)),
        'od_w_out': nrm(ks[19], (N_ODD, MIX_W, D_MODEL), MIX_W ** -0.5),
    }


def reference(x, mix_norm_g, mlp_norm_g, mlp_w1, mlp_w2, ev_w_in, ev_w_out,
              sg_ln_g, sg_ln_b, sg_w, sg_b, sb_q_norm_g, sb_k_norm_g,
              od_w_in, od_conv_w, od_conv_b, od_i_b, od_f_b, od_out_norm_g, od_w_out):
    for layer in range(DEPTH):
        h = _rmsnorm(x, mix_norm_g[layer])
        j = layer // 2
        if layer % 2 == 0:
            x = x + _even_mixer(h, ev_w_in[j], ev_w_out[j], sg_ln_g[j], sg_ln_b[j],
                                sg_w[j], sg_b[j], sb_q_norm_g[j], sb_k_norm_g[j])
        else:
            x = x + _odd_mixer(h, od_w_in[j], od_conv_w[j], od_conv_b[j], od_i_b[j],
                               od_f_b[j], od_out_norm_g[j], od_w_out[j])
        h = _rmsnorm(x, mlp_norm_g[layer])
        x = x + _sqrelu_mlp(h, mlp_w1[layer], mlp_w2[layer])
    return x
```

```python
import functools

import jax
import jax.numpy as jnp
from jax import lax
from jax.experimental import pallas as pl
from jax.experimental.pallas import tpu as pltpu

F32 = jnp.float32
BF16 = jnp.bfloat16
EPS = 1e-6

LANES = 128
CHUNK = 128
A_GROUPS = 8
A_GDIM = 64
B_HEADS = 8
B_HDIM = 64
C_HEADS = 4
C_QK = 128
C_V = 256
CONV_W = 4
C_AUG = C_V + LANES

ROW_TILE = 512
ATT_TQ = 256
ATT_TK = 256
VMEM_LIMIT = 56 * 1024 * 1024


def _dot(a, b):
    return jnp.dot(a, b, preferred_element_type=F32)


def _dot_nt(a, b):
    return lax.dot_general(a, b, (((1,), (1,)), ((), ())), preferred_element_type=F32)


def _dot_tn(a, b):
    return lax.dot_general(a, b, (((0,), (0,)), ((), ())), preferred_element_type=F32)


def _const_spec(shape):
    nd = len(shape)
    return pl.BlockSpec(shape, lambda *_: (0,) * nd, pipeline_mode=pl.Buffered(1))


def _rms_rows(x, g):
    ms = jnp.mean(x * x, axis=-1, keepdims=True)
    return x * lax.rsqrt(ms + EPS) * g


def _gelu_tanh(x):
    c = 0.7978845608028654
    return 0.5 * x * (1.0 + jnp.tanh(c * (x + 0.044715 * (x * x * x))))


def _split_bf16(x, parts):
    out = []
    r = x
    for i in range(parts):
        p = r.astype(BF16)
        out.append(p)
        if i + 1 < parts:
            r = r - p.astype(F32)
    return out


def _even_in_kernel(x_ref, g_ref, w_ref, lng_ref, lnb_ref, sgw_ref, sgb_ref,
                    ya_ref, qkv_ref):
    a_width = A_GROUPS * A_GDIM
    h = _rms_rows(x_ref[...], g_ref[...]).astype(BF16)
    p = _dot(h, w_ref[...])
    qkv_ref[...] = p[:, 2 * a_width:].astype(BF16)
    u = _gelu_tanh(p[:, :a_width])
    vg = _gelu_tanh(p[:, a_width:2 * a_width])
    mu = jnp.mean(vg, axis=-1, keepdims=True)
    vc = vg - mu
    var = jnp.mean(vc * vc, axis=-1, keepdims=True)
    vgn = (vc * lax.rsqrt(var + EPS) * lng_ref[...] + lnb_ref[...]).astype(BF16)

    r = lax.broadcasted_iota(jnp.int32, (CHUNK, 2 * CHUNK), 0)
    c = lax.broadcasted_iota(jnp.int32, (CHUNK, 2 * CHUNK), 1)
    causal = (c % CHUNK) <= r
    lane = lax.broadcasted_iota(jnp.int32, (CHUNK, LANES), 1)
    first = lane < A_GDIM
    zero = jnp.zeros((CHUNK, LANES), BF16)
    n_pairs = A_GROUPS // 2
    w_pairs = [jnp.where(causal, sgw_ref[pr], 0.0).astype(BF16) for pr in range(n_pairs)]
    for ci in range(x_ref.shape[0] // CHUNK):
        rows = slice(ci * CHUNK, (ci + 1) * CHUNK)
        mixed = []
        for pr in range(n_pairs):
            pair = vgn[rows, pr * LANES:(pr + 1) * LANES]
            rhs = jnp.concatenate([jnp.where(first, pair, zero),
                                   jnp.where(first, zero, pair)], axis=0)
            mixed.append(_dot(w_pairs[pr], rhs))
        mixed = jnp.concatenate(mixed, axis=1) + sgb_ref[...]
        ya_ref[rows, :] = (u[rows, :] * mixed).astype(BF16)


def _even_in(x2, g, w_in, ln_g, ln_b, sgw_pairs, sgb_full):
    t, d = x2.shape
    n_in = w_in.shape[1]
    a_width = A_GROUPS * A_GDIM
    n_qkv = n_in - 2 * a_width
    tm = ROW_TILE
    return pl.pallas_call(
        _even_in_kernel,
        grid=(t // tm,),
        in_specs=[
            pl.BlockSpec((tm, d), lambda i: (i, 0)),
            _const_spec((1, d)),
            _const_spec((d, n_in)),
            _const_spec((1, a_width)),
            _const_spec((1, a_width)),
            _const_spec(sgw_pairs.shape),
            _const_spec(sgb_full.shape),
        ],
        out_specs=[
            pl.BlockSpec((tm, a_width), lambda i: (i, 0)),
            pl.BlockSpec((tm, n_qkv), lambda i: (i, 0)),
        ],
        out_shape=[
            jax.ShapeDtypeStruct((t, a_width), BF16),
            jax.ShapeDtypeStruct((t, n_qkv), BF16),
        ],
        compiler_params=pltpu.CompilerParams(
            dimension_semantics=("arbitrary",), vmem_limit_bytes=VMEM_LIMIT),
        name="even_in_proj",
    )(x2, g, w_in, ln_g, ln_b, sgw_pairs, sgb_full)


def _softplus(z):
    return jnp.maximum(z, 0.0) + jnp.log(1.0 + jnp.exp(-jnp.abs(z)))


def _sb_kernel(q_ref, k_ref, v_ref, gq_ref, gk_ref, seg_ref, tri_ref, y_ref,
               q0_ref, q1_ref, kt_ref):
    seq = q_ref.shape[0]
    tq, tk = ATT_TQ, ATT_TK
    lane = lax.broadcasted_iota(jnp.int32, (tq, LANES), 1)
    first = lane < B_HDIM
    scale = B_HDIM ** -0.5

    for r0 in range(0, seq, tq):
        rows = slice(r0, r0 + tq)
        q = q_ref[rows, :].astype(F32)
        k = k_ref[rows, :].astype(F32)
        ssq = _dot((q * q).astype(BF16), seg_ref[...])
        ssk = _dot((k * k).astype(BF16), seg_ref[...])
        qn = q * lax.rsqrt(ssq * (1.0 / B_HDIM) + EPS) * (gq_ref[...] * scale)
        kn = k * lax.rsqrt(ssk * (1.0 / B_HDIM) + EPS) * gk_ref[...]
        q0_ref[rows, :] = jnp.where(first, qn, 0.0).astype(BF16)
        q1_ref[rows, :] = jnp.where(first, 0.0, qn).astype(BF16)
        kt_ref[r0 // tk] = kn.T.astype(BF16)

    rr = lax.broadcasted_iota(jnp.int32, (tq, tk), 0)
    cc = lax.broadcasted_iota(jnp.int32, (tq, tk), 1)
    strict = cc < rr

    def tile(qh, kj, carry, acc, diag):
        ks = pl.multiple_of(kj * tk, tk)
        z = _dot(qh, kt_ref[kj])
        l1m = -_softplus(z)
        if diag:
            l1m = jnp.where(strict, l1m, 0.0)
        hi, lo = _split_bf16(l1m, 2)
        cs = _dot(jnp.concatenate([hi, lo], axis=1), tri_ref[...])
        a = jnp.exp(z + cs + carry)
        if diag:
            a = jnp.where(strict, a, 0.0)
        acc = acc + _dot(a.astype(BF16), v_ref[pl.ds(ks, tk), :])
        carry = carry + cs[:, 0:1]
        return carry, acc

    def q_block(qi, _):
        qs = pl.multiple_of(qi * tq, tq)
        qa = q0_ref[pl.ds(qs, tq), :]
        qb = q1_ref[pl.ds(qs, tq), :]
        zc = jnp.zeros((tq, 1), F32)
        za = jnp.zeros((tq, LANES), F32)
        ca, aa = tile(qa, qi, zc, za, True)
        cb, ab = tile(qb, qi, zc, za, True)

        def k_block(it, st):
            ca, aa, cb, ab = st
            kj = qi - 1 - it
            ca, aa = tile(qa, kj, ca, aa, False)
            cb, ab = tile(qb, kj, cb, ab, False)
            return ca, aa, cb, ab

        ca, aa, cb, ab = lax.fori_loop(0, qi, k_block, (ca, aa, cb, ab))
        y_ref[pl.ds(qs, tq), :] = jnp.where(first, aa, ab).astype(BF16)
        return 0

    lax.fori_loop(0, seq // tq, q_block, 0)


def _stick_breaking(qkv, gq, gk, bsz, seq):
    t = qkv.shape[0]
    n_pairs = B_HEADS // 2
    seg = (jnp.arange(LANES)[:, None] // B_HDIM == jnp.arange(LANES)[None, :] // B_HDIM).astype(BF16)
    tri = (jnp.arange(ATT_TK)[:, None] >= jnp.arange(ATT_TK)[None, :]).astype(BF16)
    tri2 = jnp.concatenate([tri, tri], axis=0)
    gq2 = jnp.tile(gq.reshape(1, B_HDIM), (1, 2)).astype(F32)
    gk2 = jnp.tile(gk.reshape(1, B_HDIM), (1, 2)).astype(F32)
    return pl.pallas_call(
        _sb_kernel,
        grid=(bsz, n_pairs),
        in_specs=[
            pl.BlockSpec((seq, LANES), lambda b, p: (b, p)),
            pl.BlockSpec((seq, LANES), lambda b, p: (b, n_pairs + p)),
            pl.BlockSpec((seq, LANES), lambda b, p: (b, 2 * n_pairs + p)),
            _const_spec((1, LANES)),
            _const_spec((1, LANES)),
            _const_spec((LANES, LANES)),
            _const_spec((2 * ATT_TK, ATT_TK)),
        ],
        out_specs=pl.BlockSpec((seq, LANES), lambda b, p: (b, p)),
        out_shape=jax.ShapeDtypeStruct((t, B_HEADS * B_HDIM), BF16),
        scratch_shapes=[
            pltpu.VMEM((seq, LANES), BF16),
            pltpu.VMEM((seq, LANES), BF16),
            pltpu.VMEM((seq // ATT_TK, LANES, ATT_TK), BF16),
        ],
        compiler_params=pltpu.CompilerParams(
            dimension_semantics=("arbitrary", "arbitrary"), vmem_limit_bytes=VMEM_LIMIT),
        name="stick_breaking",
    )(qkv, qkv, qkv, gq2, gk2, seg, tri2)


def _odd_in_kernel(x_ref, g_ref, w_ref, wg_ref, p_ref, gates_ref):
    h = _rms_rows(x_ref[...], g_ref[...]).astype(BF16)
    p_ref[...] = _dot(h, w_ref[...]).astype(BF16)
    gates_ref[...] = _dot(h, wg_ref[...])


def _odd_in(x2, g, w_main, w_gates):
    t, d = x2.shape
    n_main = w_main.shape[1]
    tm = ROW_TILE
    return pl.pallas_call(
        _odd_in_kernel,
        grid=(t // tm,),
        in_specs=[
            pl.BlockSpec((tm, d), lambda i: (i, 0)),
            _const_spec((1, d)),
            _const_spec((d, n_main)),
            _const_spec((d, LANES)),
        ],
        out_specs=[
            pl.BlockSpec((tm, n_main), lambda i: (i, 0)),
            pl.BlockSpec((tm, LANES), lambda i: (i, 0)),
        ],
        out_shape=[
            jax.ShapeDtypeStruct((t, n_main), BF16),
            jax.ShapeDtypeStruct((t, LANES), F32),
        ],
        compiler_params=pltpu.CompilerParams(
            dimension_semantics=("arbitrary",), vmem_limit_bytes=VMEM_LIMIT),
        name="odd_in_proj",
    )(x2, g, w_main, w_gates)


def _mlstm_kernel(qk_ref, v_ref, og_ref, gates_ref, cw_ref, cb_ref, gb_ref, ong_ref,
                  tril3_ref, y_ref, prev_ref, c_ref, m_ref):
    n_qk = C_HEADS * C_QK
    neg = -1e30

    @pl.when(pl.program_id(1) == 0)
    def _():
        prev_ref[...] = jnp.zeros_like(prev_ref)
        c_ref[...] = jnp.zeros_like(c_ref)
        m_ref[...] = jnp.zeros_like(m_ref)

    x = qk_ref[...].astype(F32)
    ext = jnp.concatenate([prev_ref[...], x], axis=0)
    conv = cb_ref[...] + cw_ref[CONV_W - 1:CONV_W, :] * x
    for j in range(1, CONV_W):
        conv = conv + cw_ref[CONV_W - 1 - j:CONV_W - j, :] * ext[8 - j:8 - j + CHUNK, :]
    prev_ref[...] = x[CHUNK - 8:, :]
    qk = conv * jax.nn.sigmoid(conv)

    col = lax.broadcasted_iota(jnp.int32, (CHUNK, LANES), 1)
    gpre = gates_ref[...] + gb_ref[...]
    logf = jnp.minimum(gpre, 0.0) - jnp.log(1.0 + jnp.exp(-jnp.abs(gpre)))
    gx = jnp.where(col < C_HEADS, gpre, jnp.where(col < 2 * C_HEADS, logf, 0.0))
    bcum = _dot(tril3_ref[...], jnp.concatenate(_split_bf16(gx, 3), axis=0))
    gx_t = gx.T
    bcum_t = bcum.T

    rr = lax.broadcasted_iota(jnp.int32, (CHUNK, CHUNK), 0)
    cc = lax.broadcasted_iota(jnp.int32, (CHUNK, CHUNK), 1)
    tril = cc <= rr
    ones_col = (col == 0).astype(BF16)

    for h in range(C_HEADS):
        qh = qk[:, h * C_QK:(h + 1) * C_QK].astype(BF16)
        kf = qk[:, n_qk + h * C_QK:n_qk + (h + 1) * C_QK] * (C_QK ** -0.5)
        kh = kf.astype(BF16)
        vaug = jnp.concatenate([v_ref[:, h * C_V:(h + 1) * C_V], ones_col], axis=1)
        c_st = c_ref[h]
        m_st = m_ref[:, h:h + 1]

        b_col = bcum[:, C_HEADS + h:C_HEADS + h + 1]
        b_row = bcum_t[C_HEADS + h:C_HEADS + h + 1, :]
        i_col = gx[:, h:h + 1]
        i_row = gx_t[h:h + 1, :]
        dmat = jnp.where(tril, b_col - b_row + i_row, neg)
        g = b_col + m_st
        m_t = jnp.maximum(g, jnp.max(dmat, axis=-1, keepdims=True))
        wts = jnp.where(tril, jnp.exp(dmat - m_t), 0.0)
        inter = jnp.exp(g - m_t)
        s = _dot_nt(qh, kh) * wts
        nd = _dot(s.astype(BF16), vaug) + inter * _dot(qh, c_st.astype(BF16))
        den = nd[:, C_V:C_V + 1]
        hh = nd[:, :C_V] / jnp.maximum(jnp.abs(den), jnp.exp(-m_t))

        b_last = bcum[CHUNK - 1:CHUNK, C_HEADS + h:C_HEADS + h + 1]
        wk = b_last - b_col + i_col
        m_new = jnp.maximum(b_last + m_st, jnp.max(wk, axis=0, keepdims=True))
        decay = jnp.exp(b_last + m_st - m_new)
        wkx = jnp.exp(wk - m_new)
        c_ref[h] = decay * c_st + _dot_tn((kf * wkx).astype(BF16), vaug)
        m_ref[:, h:h + 1] = m_new

        ms = jnp.mean(hh * hh, axis=-1, keepdims=True)
        hn = hh * lax.rsqrt(ms + EPS) * ong_ref[:, h * C_V:(h + 1) * C_V]
        og = og_ref[:, h * C_V:(h + 1) * C_V].astype(F32)
        y_ref[:, h * C_V:(h + 1) * C_V] = (hn * jax.nn.sigmoid(og)).astype(BF16)


def _mlstm(p_main, gates, conv_w, conv_b, gate_b, on_g, bsz, seq):
    t = p_main.shape[0]
    n_qk2 = 2 * C_HEADS * C_QK
    n_v = C_HEADS * C_V
    nc = seq // CHUNK
    tril = (jnp.arange(CHUNK)[:, None] >= jnp.arange(CHUNK)[None, :]).astype(BF16)
    tril3 = jnp.concatenate([tril, tril, tril], axis=1)
    assert n_qk2 == n_v
    row = lambda b, c: (b * nc + c, 0)
    return pl.pallas_call(
        _mlstm_kernel,
        grid=(bsz, nc),
        in_specs=[
            pl.BlockSpec((CHUNK, n_qk2), lambda b, c: (b * nc + c, 0)),
            pl.BlockSpec((CHUNK, n_v), lambda b, c: (b * nc + c, 1)),
            pl.BlockSpec((CHUNK, n_v), lambda b, c: (b * nc + c, 2)),
            pl.BlockSpec((CHUNK, LANES), row),
            _const_spec((CONV_W, n_qk2)),
            _const_spec((1, n_qk2)),
            _const_spec((1, LANES)),
            _const_spec((1, n_v)),
            _const_spec((CHUNK, 3 * CHUNK)),
        ],
        out_specs=pl.BlockSpec((CHUNK, n_v), row),
        out_shape=jax.ShapeDtypeStruct((t, n_v), BF16),
        scratch_shapes=[
            pltpu.VMEM((8, n_qk2), F32),
            pltpu.VMEM((C_HEADS, C_QK, C_AUG), F32),
            pltpu.VMEM((1, LANES), F32),
        ],
        compiler_params=pltpu.CompilerParams(
            dimension_semantics=("arbitrary", "arbitrary"), vmem_limit_bytes=VMEM_LIMIT),
        name="mlstm",
    )(p_main, p_main, p_main, gates, conv_w, conv_b, gate_b, on_g, tril3)


def _out_mlp_kernel(n_y, *refs):
    x_ref = refs[0]
    y_refs = refs[1:1 + n_y]
    wo_ref, g_ref, w1_ref, w2_ref, o_ref = refs[1 + n_y:]
    d = x_ref.shape[1]
    y = jnp.concatenate([r[...] for r in y_refs], axis=1) if n_y > 1 else y_refs[0][...]
    x1 = x_ref[...] + _dot(y, wo_ref[...])
    h = _rms_rows(x1, g_ref[...]).astype(BF16)
    acc = x1
    for c0 in range(0, w1_ref.shape[1], d):
        a = jnp.maximum(_dot(h, w1_ref[:, c0:c0 + d]), 0.0)
        acc = acc + _dot((a * a).astype(BF16), w2_ref[c0:c0 + d, :])
    o_ref[...] = acc


def _out_mlp(x2, ys, w_out, g, w1, w2):
    t, d = x2.shape
    d_ff = w1.shape[1]
    tm = ROW_TILE
    in_specs = [pl.BlockSpec((tm, d), lambda i: (i, 0))]
    in_specs += [pl.BlockSpec((tm, y.shape[1]), lambda i: (i, 0)) for y in ys]
    in_specs += [_const_spec((d, d)), _const_spec((1, d)), _const_spec((d, d_ff)),
                 _const_spec((d_ff, d))]
    return pl.pallas_call(
        functools.partial(_out_mlp_kernel, len(ys)),
        grid=(t // tm,),
        in_specs=in_specs,
        out_specs=pl.BlockSpec((tm, d), lambda i: (i, 0)),
        out_shape=jax.ShapeDtypeStruct((t, d), F32),
        input_output_aliases={0: 0},
        compiler_params=pltpu.CompilerParams(
            dimension_semantics=("arbitrary",), vmem_limit_bytes=VMEM_LIMIT),
        name="out_proj_mlp",
    )(x2, *ys, w_out, g, w1, w2)


def kernel(x, mix_norm_g, mlp_norm_g, mlp_w1, mlp_w2, ev_w_in, ev_w_out, sg_ln_g, sg_ln_b, sg_w, sg_b, sb_q_norm_g, sb_k_norm_g, od_w_in, od_conv_w, od_conv_b, od_i_b, od_f_b, od_out_norm_g, od_w_out):
    bsz, seq, d = x.shape
    depth = mix_norm_g.shape[0]
    a_width = A_GROUPS * A_GDIM
    n_main = 2 * C_HEADS * C_QK + 2 * C_HEADS * C_V
    x2 = x.reshape(bsz * seq, d)
    row = lambda v: v.reshape(1, -1).astype(F32)
    for layer in range(depth):
        j = layer // 2
        g_mix = row(mix_norm_g[layer])
        if layer % 2 == 0:
            sgw_pairs = jnp.concatenate([sg_w[j, 0::2], sg_w[j, 1::2]], axis=-1)
            sgb_full = jnp.repeat(sg_b[j].T, A_GDIM, axis=1)
            ya, qkv = _even_in(x2, g_mix, ev_w_in[j].astype(BF16), row(sg_ln_g[j]),
                               row(sg_ln_b[j]), sgw_pairs, sgb_full)
            yb = _stick_breaking(qkv, sb_q_norm_g[j], sb_k_norm_g[j], bsz, seq)
            ys = (ya, yb)
            w_out = ev_w_out[j]
        else:
            w_in = od_w_in[j]
            w_gates = jnp.pad(w_in[:, n_main:], ((0, 0), (0, LANES - 2 * C_HEADS)))
            p_main, gates = _odd_in(x2, g_mix, w_in[:, :n_main].astype(BF16),
                                    w_gates.astype(BF16))
            gate_b = jnp.pad(jnp.concatenate([od_i_b[j], od_f_b[j]]),
                             (0, LANES - 2 * C_HEADS)).reshape(1, LANES)
            y = _mlstm(p_main, gates, od_conv_w[j], row(od_conv_b[j]), gate_b,
                       row(od_out_norm_g[j]), bsz, seq)
            ys = (y,)
            w_out = od_w_out[j]
        x2 = _out_mlp(x2, ys, w_out.astype(BF16), row(mlp_norm_g[layer]),
                      mlp_w1[layer].astype(BF16), mlp_w2[layer].astype(BF16))
    return x2.reshape(bsz, seq, d)
```

```python
import functools

import jax
import jax.numpy as jnp
from jax import lax
from jax.experimental import pallas as pl
from jax.experimental.pallas import tpu as pltpu

F32 = jnp.float32
BF16 = jnp.bfloat16
EPS = 1e-6

LANES = 128
CHUNK = 128
A_GROUPS = 8
A_GDIM = 64
B_HEADS = 8
B_HDIM = 64
C_HEADS = 4
C_QK = 128
C_V = 256
CONV_W = 4
C_AUG = C_V + LANES

ROW_TILE = 512
ATT_TQ = 256
ATT_TK = 256
ATT_GROUP = 2
VMEM_LIMIT = 56 * 1024 * 1024
LOG2E = 1.4426950408889634


def _dot(a, b):
    return jnp.dot(a, b, preferred_element_type=F32)


def _dot_nt(a, b):
    return lax.dot_general(a, b, (((1,), (1,)), ((), ())), preferred_element_type=F32)


def _dot_tn(a, b):
    return lax.dot_general(a, b, (((0,), (0,)), ((), ())), preferred_element_type=F32)


def _const_spec(shape):
    nd = len(shape)
    return pl.BlockSpec(shape, lambda *_: (0,) * nd, pipeline_mode=pl.Buffered(1))


def _rms_rows(x, g):
    ms = jnp.mean(x * x, axis=-1, keepdims=True)
    return x * lax.rsqrt(ms + EPS) * g


def _gelu_tanh(x):
    c = 0.7978845608028654
    return 0.5 * x * (1.0 + jnp.tanh(c * (x + 0.044715 * (x * x * x))))


def _split_bf16(x, parts):
    out = []
    r = x
    for i in range(parts):
        p = r.astype(BF16)
        out.append(p)
        if i + 1 < parts:
            r = r - p.astype(F32)
    return out


def _even_in_kernel(x_ref, g_ref, w_ref, lng_ref, lnb_ref, sgw_ref, sgb_ref,
                    ya_ref, qkv_ref):
    a_width = A_GROUPS * A_GDIM
    h = _rms_rows(x_ref[...], g_ref[...]).astype(BF16)
    p = _dot(h, w_ref[...])
    qkv_ref[...] = p[:, 2 * a_width:].astype(BF16)
    u = _gelu_tanh(p[:, :a_width])
    vg = _gelu_tanh(p[:, a_width:2 * a_width])
    mu = jnp.mean(vg, axis=-1, keepdims=True)
    vc = vg - mu
    var = jnp.mean(vc * vc, axis=-1, keepdims=True)
    vgn = (vc * lax.rsqrt(var + EPS) * lng_ref[...] + lnb_ref[...]).astype(BF16)

    r = lax.broadcasted_iota(jnp.int32, (CHUNK, 2 * CHUNK), 0)
    c = lax.broadcasted_iota(jnp.int32, (CHUNK, 2 * CHUNK), 1)
    causal = (c % CHUNK) <= r
    lane = lax.broadcasted_iota(jnp.int32, (CHUNK, LANES), 1)
    first = lane < A_GDIM
    zero = jnp.zeros((CHUNK, LANES), BF16)
    n_pairs = A_GROUPS // 2
    w_pairs = [jnp.where(causal, sgw_ref[pr], 0.0).astype(BF16) for pr in range(n_pairs)]
    for ci in range(x_ref.shape[0] // CHUNK):
        rows = slice(ci * CHUNK, (ci + 1) * CHUNK)
        mixed = []
        for pr in range(n_pairs):
            pair = vgn[rows, pr * LANES:(pr + 1) * LANES]
            rhs = jnp.concatenate([jnp.where(first, pair, zero),
                                   jnp.where(first, zero, pair)], axis=0)
            mixed.append(_dot(w_pairs[pr], rhs))
        mixed = jnp.concatenate(mixed, axis=1) + sgb_ref[...]
        ya_ref[rows, :] = (u[rows, :] * mixed).astype(BF16)


def _even_in(x2, g, w_in, ln_g, ln_b, sgw_pairs, sgb_full):
    t, d = x2.shape
    n_in = w_in.shape[1]
    a_width = A_GROUPS * A_GDIM
    n_qkv = n_in - 2 * a_width
    tm = ROW_TILE
    return pl.pallas_call(
        _even_in_kernel,
        grid=(t // tm,),
        in_specs=[
            pl.BlockSpec((tm, d), lambda i: (i, 0)),
            _const_spec((1, d)),
            _const_spec((d, n_in)),
            _const_spec((1, a_width)),
            _const_spec((1, a_width)),
            _const_spec(sgw_pairs.shape),
            _const_spec(sgb_full.shape),
        ],
        out_specs=[
            pl.BlockSpec((tm, a_width), lambda i: (i, 0)),
            pl.BlockSpec((tm, n_qkv), lambda i: (i, 0)),
        ],
        out_shape=[
            jax.ShapeDtypeStruct((t, a_width), BF16),
            jax.ShapeDtypeStruct((t, n_qkv), BF16),
        ],
        compiler_params=pltpu.CompilerParams(
            dimension_semantics=("arbitrary",), vmem_limit_bytes=VMEM_LIMIT),
        name="even_in_proj",
    )(x2, g, w_in, ln_g, ln_b, sgw_pairs, sgb_full)


def _sb_kernel(qi_tab, kj_tab, q_ref, k_ref, v_ref, gq_ref, gk_ref, seg_ref, ntri_ref,
               bias_ref, y_ref, q0_ref, q1_ref, kt_ref, z_a, z_b, hl_a, hl_b, a_a, a_b,
               acc_ref, c_ref):
    seq = q_ref.shape[0]
    tq, tk = ATT_TQ, ATT_TK
    nq = seq // tq
    n_tiles = nq * (nq + 1) // 2
    lane = lax.broadcasted_iota(jnp.int32, (tq, LANES), 1)
    first = lane < B_HDIM
    scale = B_HDIM ** -0.5 * LOG2E

    for r0 in range(0, seq, tq):
        rows = slice(r0, r0 + tq)
        q = q_ref[rows, :].astype(F32)
        k = k_ref[rows, :].astype(F32)
        ssq = _dot((q * q).astype(BF16), seg_ref[...])
        ssk = _dot((k * k).astype(BF16), seg_ref[...])
        qn = q * lax.rsqrt(ssq * (1.0 / B_HDIM) + EPS) * (gq_ref[...] * scale)
        kn = k * lax.rsqrt(ssk * (1.0 / B_HDIM) + EPS) * gk_ref[...]
        q0_ref[rows, :] = jnp.where(first, qn, 0.0).astype(BF16)
        q1_ref[rows, :] = jnp.where(first, 0.0, qn).astype(BF16)
        kt_ref[r0 // tk] = kn.T.astype(BF16)

    acc_ref[...] = jnp.zeros_like(acc_ref)
    c_ref[...] = jnp.zeros_like(c_ref)

    z_bufs, hl_bufs, a_bufs = (z_a, z_b), (hl_a, hl_b), (a_a, a_b)
    n_groups = qi_tab.shape[0] // ATT_GROUP
    heads = range(2)

    def logits(n, z_buf, hl_buf, u, h):
        qi = qi_tab[n]
        kj = kj_tab[n]
        qs = pl.multiple_of(qi * tq, tq)
        bias = bias_ref[(qi == kj).astype(jnp.int32)]
        qh_ref = (q0_ref, q1_ref)[h]
        z = _dot(qh_ref[pl.ds(qs, tq), :], kt_ref[kj]) + bias
        z_buf[u, h] = z
        sp = jnp.maximum(z, jnp.log(1.0 + jnp.exp2(jnp.minimum(z, 64.0))) * LOG2E)
        hi, lo = _split_bf16(sp, 2)
        hl_buf[u, h, :, :tk] = hi
        hl_buf[u, h, :, tk:] = lo

    def weights(n, z_buf, hl_buf, a_buf, u, h):
        fresh = qi_tab[n] == kj_tab[n]
        cs = _dot(hl_buf[u, h], ntri_ref[...])
        c = jnp.where(fresh, 0.0, c_ref[h])
        la = z_buf[u, h] + cs + jnp.concatenate([c] * (tk // LANES), axis=1)
        a_buf[u, h] = jnp.exp2(la).astype(BF16)
        c_ref[h] = c + jnp.broadcast_to(cs[:, 0:1], (tq, LANES))

    def values(n, a_buf, u, h):
        qi = qi_tab[n]
        kj = kj_tab[n]
        qs = pl.multiple_of(qi * tq, tq)
        ks = pl.multiple_of(kj * tk, tk)
        acc = jnp.where(qi == kj, 0.0, acc_ref[h]) + _dot(a_buf[u, h], v_ref[pl.ds(ks, tk), :])
        acc_ref[h] = acc
        lanes = slice(h * B_HDIM, (h + 1) * B_HDIM)
        y_ref[pl.ds(qs, tq), lanes] = acc[:, lanes].astype(BF16)

    def step(tau, par):
        static = isinstance(tau, int)
        ok = lambda g: not static or 0 <= g < n_groups
        for u in range(ATT_GROUP):
            for h in heads:
                if ok(tau):
                    logits(tau * ATT_GROUP + u, z_bufs[par], hl_bufs[par], u, h)
                if ok(tau - 1):
                    weights((tau - 1) * ATT_GROUP + u, z_bufs[1 - par], hl_bufs[1 - par],
                            a_bufs[1 - par], u, h)
                if ok(tau - 2):
                    values((tau - 2) * ATT_GROUP + u, a_bufs[par], u, h)

    for tau in range(2):
        step(tau, tau % 2)

    def body(i, _):
        tau = 2 + 2 * i
        step(tau, 0)
        step(tau + 1, 1)
        return 0

    n_body = max(n_groups - 2, 0) // 2
    lax.fori_loop(0, n_body, body, 0)
    for tau in range(2 + 2 * n_body, n_groups + 2):
        step(tau, tau % 2)


def _stick_breaking(qkv, gq, gk, bsz, seq):
    t = qkv.shape[0]
    n_pairs = B_HEADS // 2
    tq, tk = ATT_TQ, ATT_TK
    assert tq == tk and seq % tq == 0 and seq // tq >= 2
    nq = seq // tq
    tiles = [(qi, kj) for qi in range(nq) for kj in range(qi, -1, -1)]
    tiles += [(0, 0)] * (-len(tiles) % ATT_GROUP)
    qi_tab = jnp.asarray([t_[0] for t_ in tiles], jnp.int32)
    kj_tab = jnp.asarray([t_[1] for t_ in tiles], jnp.int32)
    seg = (jnp.arange(LANES)[:, None] // B_HDIM == jnp.arange(LANES)[None, :] // B_HDIM).astype(BF16)
    tri = (jnp.arange(tk)[:, None] >= jnp.arange(tk)[None, :]).astype(BF16)
    ntri2 = -jnp.concatenate([tri, tri], axis=0)
    strict = jnp.arange(tk)[None, :] < jnp.arange(tq)[:, None]
    bias = jnp.stack([jnp.zeros((tq, tk), F32), jnp.where(strict, 0.0, -1e30).astype(F32)])
    gq2 = jnp.tile(gq.reshape(1, B_HDIM), (1, 2)).astype(F32)
    gk2 = jnp.tile(gk.reshape(1, B_HDIM), (1, 2)).astype(F32)
    const = lambda shape: pl.BlockSpec(shape, lambda *_: (0,) * len(shape),
                                       pipeline_mode=pl.Buffered(1))
    grid_spec = pltpu.PrefetchScalarGridSpec(
        num_scalar_prefetch=2,
        grid=(bsz, n_pairs),
        in_specs=[
            pl.BlockSpec((seq, LANES), lambda b, p, *_: (b, p)),
            pl.BlockSpec((seq, LANES), lambda b, p, *_: (b, n_pairs + p)),
            pl.BlockSpec((seq, LANES), lambda b, p, *_: (b, 2 * n_pairs + p)),
            const((1, LANES)),
            const((1, LANES)),
            const((LANES, LANES)),
            const((2 * tk, tk)),
            const((2, tq, tk)),
        ],
        out_specs=pl.BlockSpec((seq, LANES), lambda b, p, *_: (b, p)),
        scratch_shapes=[
            pltpu.VMEM((seq, LANES), BF16),
            pltpu.VMEM((seq, LANES), BF16),
            pltpu.VMEM((seq // tk, LANES, tk), BF16),
            pltpu.VMEM((ATT_GROUP, 2, tq, tk), F32),
            pltpu.VMEM((ATT_GROUP, 2, tq, tk), F32),
            pltpu.VMEM((ATT_GROUP, 2, tq, 2 * tk), BF16),
            pltpu.VMEM((ATT_GROUP, 2, tq, 2 * tk), BF16),
            pltpu.VMEM((ATT_GROUP, 2, tq, tk), BF16),
            pltpu.VMEM((ATT_GROUP, 2, tq, tk), BF16),
            pltpu.VMEM((2, tq, LANES), F32),
            pltpu.VMEM((2, tq, LANES), F32),
        ],
    )
    return pl.pallas_call(
        _sb_kernel,
        grid_spec=grid_spec,
        out_shape=jax.ShapeDtypeStruct((t, B_HEADS * B_HDIM), BF16),
        compiler_params=pltpu.CompilerParams(
            dimension_semantics=("arbitrary", "arbitrary"), vmem_limit_bytes=VMEM_LIMIT),
        name="stick_breaking",
    )(qi_tab, kj_tab, qkv, qkv, qkv, gq2, gk2, seg, ntri2, bias)


def _odd_in_kernel(x_ref, g_ref, w_ref, wg_ref, p_ref, gates_ref):
    h = _rms_rows(x_ref[...], g_ref[...]).astype(BF16)
    p_ref[...] = _dot(h, w_ref[...]).astype(BF16)
    gates_ref[...] = _dot(h, wg_ref[...])


def _odd_in(x2, g, w_main, w_gates):
    t, d = x2.shape
    n_main = w_main.shape[1]
    tm = ROW_TILE
    return pl.pallas_call(
        _odd_in_kernel,
        grid=(t // tm,),
        in_specs=[
            pl.BlockSpec((tm, d), lambda i: (i, 0)),
            _const_spec((1, d)),
            _const_spec((d, n_main)),
            _const_spec((d, LANES)),
        ],
        out_specs=[
            pl.BlockSpec((tm, n_main), lambda i: (i, 0)),
            pl.BlockSpec((tm, LANES), lambda i: (i, 0)),
        ],
        out_shape=[
            jax.ShapeDtypeStruct((t, n_main), BF16),
            jax.ShapeDtypeStruct((t, LANES), F32),
        ],
        compiler_params=pltpu.CompilerParams(
            dimension_semantics=("arbitrary",), vmem_limit_bytes=VMEM_LIMIT),
        name="odd_in_proj",
    )(x2, g, w_main, w_gates)


def _mlstm_kernel(qk_ref, v_ref, og_ref, gates_ref, cw_ref, cb_ref, gb_ref, ong_ref,
                  tril3_ref, y_ref, prev_ref, c_ref, m_ref):
    n_qk = C_HEADS * C_QK
    neg = -1e30

    @pl.when(pl.program_id(1) == 0)
    def _():
        prev_ref[...] = jnp.zeros_like(prev_ref)
        c_ref[...] = jnp.zeros_like(c_ref)
        m_ref[...] = jnp.zeros_like(m_ref)

    x = qk_ref[...].astype(F32)
    ext = jnp.concatenate([prev_ref[...], x], axis=0)
    conv = cb_ref[...] + cw_ref[CONV_W - 1:CONV_W, :] * x
    for j in range(1, CONV_W):
        conv = conv + cw_ref[CONV_W - 1 - j:CONV_W - j, :] * ext[8 - j:8 - j + CHUNK, :]
    prev_ref[...] = x[CHUNK - 8:, :]
    qk = conv * jax.nn.sigmoid(conv)

    col = lax.broadcasted_iota(jnp.int32, (CHUNK, LANES), 1)
    gpre = gates_ref[...] + gb_ref[...]
    logf = jnp.minimum(gpre, 0.0) - jnp.log(1.0 + jnp.exp(-jnp.abs(gpre)))
    gx = jnp.where(col < C_HEADS, gpre, jnp.where(col < 2 * C_HEADS, logf, 0.0))
    bcum = _dot(tril3_ref[...], jnp.concatenate(_split_bf16(gx, 3), axis=0))
    gx_t = gx.T
    bcum_t = bcum.T

    rr = lax.broadcasted_iota(jnp.int32, (CHUNK, CHUNK), 0)
    cc = lax.broadcasted_iota(jnp.int32, (CHUNK, CHUNK), 1)
    tril = cc <= rr
    ones_col = (col == 0).astype(BF16)

    for h in range(C_HEADS):
        qh = qk[:, h * C_QK:(h + 1) * C_QK].astype(BF16)
        kf = qk[:, n_qk + h * C_QK:n_qk + (h + 1) * C_QK] * (C_QK ** -0.5)
        kh = kf.astype(BF16)
        vaug = jnp.concatenate([v_ref[:, h * C_V:(h + 1) * C_V], ones_col], axis=1)
        c_st = c_ref[h]
        m_st = m_ref[:, h:h + 1]

        b_col = bcum[:, C_HEADS + h:C_HEADS + h + 1]
        b_row = bcum_t[C_HEADS + h:C_HEADS + h + 1, :]
        i_col = gx[:, h:h + 1]
        i_row = gx_t[h:h + 1, :]
        dmat = jnp.where(tril, b_col - b_row + i_row, neg)
        g = b_col + m_st
        m_t = jnp.maximum(g, jnp.max(dmat, axis=-1, keepdims=True))
        wts = jnp.where(tril, jnp.exp(dmat - m_t), 0.0)
        inter = jnp.exp(g - m_t)
        s = _dot_nt(qh, kh) * wts
        nd = _dot(s.astype(BF16), vaug) + inter * _dot(qh, c_st.astype(BF16))
        den = nd[:, C_V:C_V + 1]
        hh = nd[:, :C_V] / jnp.maximum(jnp.abs(den), jnp.exp(-m_t))

        b_last = bcum[CHUNK - 1:CHUNK, C_HEADS + h:C_HEADS + h + 1]
        wk = b_last - b_col + i_col
        m_new = jnp.maximum(b_last + m_st, jnp.max(wk, axis=0, keepdims=True))
        decay = jnp.exp(b_last + m_st - m_new)
        wkx = jnp.exp(wk - m_new)
        c_ref[h] = decay * c_st + _dot_tn((kf * wkx).astype(BF16), vaug)
        m_ref[:, h:h + 1] = m_new

        ms = jnp.mean(hh * hh, axis=-1, keepdims=True)
        hn = hh * lax.rsqrt(ms + EPS) * ong_ref[:, h * C_V:(h + 1) * C_V]
        og = og_ref[:, h * C_V:(h + 1) * C_V].astype(F32)
        y_ref[:, h * C_V:(h + 1) * C_V] = (hn * jax.nn.sigmoid(og)).astype(BF16)


def _mlstm(p_main, gates, conv_w, conv_b, gate_b, on_g, bsz, seq):
    t = p_main.shape[0]
    n_qk2 = 2 * C_HEADS * C_QK
    n_v = C_HEADS * C_V
    nc = seq // CHUNK
    tril = (jnp.arange(CHUNK)[:, None] >= jnp.arange(CHUNK)[None, :]).astype(BF16)
    tril3 = jnp.concatenate([tril, tril, tril], axis=1)
    assert n_qk2 == n_v
    row = lambda b, c: (b * nc + c, 0)
    return pl.pallas_call(
        _mlstm_kernel,
        grid=(bsz, nc),
        in_specs=[
            pl.BlockSpec((CHUNK, n_qk2), lambda b, c: (b * nc + c, 0)),
            pl.BlockSpec((CHUNK, n_v), lambda b, c: (b * nc + c, 1)),
            pl.BlockSpec((CHUNK, n_v), lambda b, c: (b * nc + c, 2)),
            pl.BlockSpec((CHUNK, LANES), row),
            _const_spec((CONV_W, n_qk2)),
            _const_spec((1, n_qk2)),
            _const_spec((1, LANES)),
            _const_spec((1, n_v)),
            _const_spec((CHUNK, 3 * CHUNK)),
        ],
        out_specs=pl.BlockSpec((CHUNK, n_v), row),
        out_shape=jax.ShapeDtypeStruct((t, n_v), BF16),
        scratch_shapes=[
            pltpu.VMEM((8, n_qk2), F32),
            pltpu.VMEM((C_HEADS, C_QK, C_AUG), F32),
            pltpu.VMEM((1, LANES), F32),
        ],
        compiler_params=pltpu.CompilerParams(
            dimension_semantics=("arbitrary", "arbitrary"), vmem_limit_bytes=VMEM_LIMIT),
        name="mlstm",
    )(p_main, p_main, p_main, gates, conv_w, conv_b, gate_b, on_g, tril3)


def _out_mlp_kernel(n_y, *refs):
    x_ref = refs[0]
    y_refs = refs[1:1 + n_y]
    wo_ref, g_ref, w1_ref, w2_ref, o_ref = refs[1 + n_y:]
    d = x_ref.shape[1]
    y = jnp.concatenate([r[...] for r in y_refs], axis=1) if n_y > 1 else y_refs[0][...]
    x1 = x_ref[...] + _dot(y, wo_ref[...])
    h = _rms_rows(x1, g_ref[...]).astype(BF16)
    acc = x1
    for c0 in range(0, w1_ref.shape[1], d):
        a = jnp.maximum(_dot(h, w1_ref[:, c0:c0 + d]), 0.0)
        acc = acc + _dot((a * a).astype(BF16), w2_ref[c0:c0 + d, :])
    o_ref[...] = acc


def _out_mlp(x2, ys, w_out, g, w1, w2):
    t, d = x2.shape
    d_ff = w1.shape[1]
    tm = ROW_TILE
    in_specs = [pl.BlockSpec((tm, d), lambda i: (i, 0))]
    in_specs += [pl.BlockSpec((tm, y.shape[1]), lambda i: (i, 0)) for y in ys]
    in_specs += [_const_spec((d, d)), _const_spec((1, d)), _const_spec((d, d_ff)),
                 _const_spec((d_ff, d))]
    return pl.pallas_call(
        functools.partial(_out_mlp_kernel, len(ys)),
        grid=(t // tm,),
        in_specs=in_specs,
        out_specs=pl.BlockSpec((tm, d), lambda i: (i, 0)),
        out_shape=jax.ShapeDtypeStruct((t, d), F32),
        compiler_params=pltpu.CompilerParams(
            dimension_semantics=("arbitrary",), vmem_limit_bytes=VMEM_LIMIT),
        name="out_proj_mlp",
    )(x2, *ys, w_out, g, w1, w2)


def kernel(x, mix_norm_g, mlp_norm_g, mlp_w1, mlp_w2, ev_w_in, ev_w_out, sg_ln_g, sg_ln_b, sg_w, sg_b, sb_q_norm_g, sb_k_norm_g, od_w_in, od_conv_w, od_conv_b, od_i_b, od_f_b, od_out_norm_g, od_w_out):
    bsz, seq, d = x.shape
    depth = mix_norm_g.shape[0]
    a_width = A_GROUPS * A_GDIM
    n_main = 2 * C_HEADS * C_QK + 2 * C_HEADS * C_V
    x2 = x.reshape(bsz * seq, d)
    row = lambda v: v.reshape(1, -1).astype(F32)
    for layer in range(depth):
        j = layer // 2
        g_mix = row(mix_norm_g[layer])
        if layer % 2 == 0:
            sgw_pairs = jnp.concatenate([sg_w[j, 0::2], sg_w[j, 1::2]], axis=-1)
            sgb_full = jnp.repeat(sg_b[j].T, A_GDIM, axis=1)
            ya, qkv = _even_in(x2, g_mix, ev_w_in[j].astype(BF16), row(sg_ln_g[j]),
                               row(sg_ln_b[j]), sgw_pairs, sgb_full)
            yb = _stick_breaking(qkv, sb_q_norm_g[j], sb_k_norm_g[j], bsz, seq)
            ys = (ya, yb)
            w_out = ev_w_out[j]
        else:
            w_in = od_w_in[j]
            w_gates = jnp.pad(w_in[:, n_main:], ((0, 0), (0, LANES - 2 * C_HEADS)))
            p_main, gates = _odd_in(x2, g_mix, w_in[:, :n_main].astype(BF16),
                                    w_gates.astype(BF16))
            gate_b = jnp.pad(jnp.concatenate([od_i_b[j], od_f_b[j]]),
                             (0, LANES - 2 * C_HEADS)).reshape(1, LANES)
            y = _mlstm(p_main, gates, od_conv_w[j], row(od_conv_b[j]), gate_b,
                       row(od_out_norm_g[j]), bsz, seq)
            ys = (y,)
            w_out = od_w_out[j]
        x2 = _out_mlp(x2, ys, w_out.astype(BF16), row(mlp_norm_g[layer]),
                      mlp_w1[layer].astype(BF16), mlp_w2[layer].astype(BF16))
    return x2.reshape(bsz, seq, d)
```

```python
import functools

import jax
import jax.numpy as jnp
from jax import lax
from jax.experimental import pallas as pl
from jax.experimental.pallas import tpu as pltpu

F32 = jnp.float32
BF16 = jnp.bfloat16
EPS = 1e-6

LANES = 128
CHUNK = 128
A_GROUPS = 8
A_GDIM = 64
B_HEADS = 8
B_HDIM = 64
C_HEADS = 4
C_QK = 128
C_V = 256
CONV_W = 4
C_AUG = C_V + LANES

ROW_TILE = 512
ATT_TQ = 256
ATT_TK = 256
ATT_GROUP = 2
MLSTM_CHUNKS = 4
CONV_PAD = 16
VMEM_LIMIT = 56 * 1024 * 1024
LOG2E = 1.4426950408889634


def _dot(a, b):
    return jnp.dot(a, b, preferred_element_type=F32)


def _dot_nt(a, b):
    return lax.dot_general(a, b, (((1,), (1,)), ((), ())), preferred_element_type=F32)


def _dot_tn(a, b):
    return lax.dot_general(a, b, (((0,), (0,)), ((), ())), preferred_element_type=F32)


def _const_spec(shape):
    nd = len(shape)
    return pl.BlockSpec(shape, lambda *_: (0,) * nd, pipeline_mode=pl.Buffered(1))


def _rms_rows(x, g):
    ms = jnp.mean(x * x, axis=-1, keepdims=True)
    return x * lax.rsqrt(ms + EPS) * g


def _gelu_tanh(x):
    c = 0.7978845608028654
    return 0.5 * x * (1.0 + jnp.tanh(c * (x + 0.044715 * (x * x * x))))


def _split_bf16(x, parts):
    out = []
    r = x
    for i in range(parts):
        p = r.astype(BF16)
        out.append(p)
        if i + 1 < parts:
            r = r - p.astype(F32)
    return out


def _even_in_kernel(x_ref, g_ref, w_ref, lng_ref, lnb_ref, sgw_ref, sgb_ref,
                    ya_ref, qkv_ref):
    a_width = A_GROUPS * A_GDIM
    h = _rms_rows(x_ref[...], g_ref[...]).astype(BF16)
    p = _dot(h, w_ref[...])
    qkv_ref[...] = p[:, 2 * a_width:].astype(BF16)
    u = _gelu_tanh(p[:, :a_width])
    vg = _gelu_tanh(p[:, a_width:2 * a_width])
    mu = jnp.mean(vg, axis=-1, keepdims=True)
    vc = vg - mu
    var = jnp.mean(vc * vc, axis=-1, keepdims=True)
    vgn = (vc * lax.rsqrt(var + EPS) * lng_ref[...] + lnb_ref[...]).astype(BF16)

    r = lax.broadcasted_iota(jnp.int32, (CHUNK, 2 * CHUNK), 0)
    c = lax.broadcasted_iota(jnp.int32, (CHUNK, 2 * CHUNK), 1)
    causal = (c % CHUNK) <= r
    lane = lax.broadcasted_iota(jnp.int32, (CHUNK, LANES), 1)
    first = lane < A_GDIM
    zero = jnp.zeros((CHUNK, LANES), BF16)
    n_pairs = A_GROUPS // 2
    w_pairs = [jnp.where(causal, sgw_ref[pr], 0.0).astype(BF16) for pr in range(n_pairs)]
    for ci in range(x_ref.shape[0] // CHUNK):
        rows = slice(ci * CHUNK, (ci + 1) * CHUNK)
        mixed = []
        for pr in range(n_pairs):
            pair = vgn[rows, pr * LANES:(pr + 1) * LANES]
            rhs = jnp.concatenate([jnp.where(first, pair, zero),
                                   jnp.where(first, zero, pair)], axis=0)
            mixed.append(_dot(w_pairs[pr], rhs))
        mixed = jnp.concatenate(mixed, axis=1) + sgb_ref[...]
        ya_ref[rows, :] = (u[rows, :] * mixed).astype(BF16)


def _even_in(x2, g, w_in, ln_g, ln_b, sgw_pairs, sgb_full):
    t, d = x2.shape
    n_in = w_in.shape[1]
    a_width = A_GROUPS * A_GDIM
    n_qkv = n_in - 2 * a_width
    tm = ROW_TILE
    return pl.pallas_call(
        _even_in_kernel,
        grid=(t // tm,),
        in_specs=[
            pl.BlockSpec((tm, d), lambda i: (i, 0)),
            _const_spec((1, d)),
            _const_spec((d, n_in)),
            _const_spec((1, a_width)),
            _const_spec((1, a_width)),
            _const_spec(sgw_pairs.shape),
            _const_spec(sgb_full.shape),
        ],
        out_specs=[
            pl.BlockSpec((tm, a_width), lambda i: (i, 0)),
            pl.BlockSpec((tm, n_qkv), lambda i: (i, 0)),
        ],
        out_shape=[
            jax.ShapeDtypeStruct((t, a_width), BF16),
            jax.ShapeDtypeStruct((t, n_qkv), BF16),
        ],
        compiler_params=pltpu.CompilerParams(
            dimension_semantics=("arbitrary",), vmem_limit_bytes=VMEM_LIMIT),
        name="even_in_proj",
    )(x2, g, w_in, ln_g, ln_b, sgw_pairs, sgb_full)


def _sb_kernel(qi_tab, kj_tab, q_ref, k_ref, v_ref, gq_ref, gk_ref, seg_ref, ntri_ref,
               bias_ref, y_ref, q0_ref, q1_ref, kt_ref, z_a, z_b, sp_a, sp_b, a_a, a_b,
               acc_ref, c_ref):
    seq = q_ref.shape[0]
    tq, tk = ATT_TQ, ATT_TK
    nq = seq // tq
    n_tiles = nq * (nq + 1) // 2
    lane = lax.broadcasted_iota(jnp.int32, (tq, LANES), 1)
    first = lane < B_HDIM
    scale = B_HDIM ** -0.5 * LOG2E

    for r0 in range(0, seq, tq):
        rows = slice(r0, r0 + tq)
        q = q_ref[rows, :].astype(F32)
        k = k_ref[rows, :].astype(F32)
        ssq = _dot((q * q).astype(BF16), seg_ref[...])
        ssk = _dot((k * k).astype(BF16), seg_ref[...])
        qn = q * lax.rsqrt(ssq * (1.0 / B_HDIM) + EPS) * (gq_ref[...] * scale)
        kn = k * lax.rsqrt(ssk * (1.0 / B_HDIM) + EPS) * gk_ref[...]
        q0_ref[rows, :] = jnp.where(first, qn, 0.0).astype(BF16)
        q1_ref[rows, :] = jnp.where(first, 0.0, qn).astype(BF16)
        kt_ref[r0 // tk] = kn.T.astype(BF16)

    acc_ref[...] = jnp.zeros_like(acc_ref)
    c_ref[...] = jnp.zeros_like(c_ref)

    z_bufs, sp_bufs, a_bufs = (z_a, z_b), (sp_a, sp_b), (a_a, a_b)
    n_groups = qi_tab.shape[0] // ATT_GROUP
    heads = range(2)

    def logits(n, z_buf, sp_buf, u, h):
        qi = qi_tab[n]
        kj = kj_tab[n]
        qs = pl.multiple_of(qi * tq, tq)
        bias = bias_ref[(qi == kj).astype(jnp.int32)]
        qh_ref = (q0_ref, q1_ref)[h]
        z = _dot(qh_ref[pl.ds(qs, tq), :], kt_ref[kj]) + bias
        z_buf[u, h] = z
        sp = jnp.maximum(z, jnp.log(1.0 + jnp.exp2(jnp.minimum(z, 64.0))) * LOG2E)
        sp_buf[u, h] = sp.astype(BF16)

    def weights(n, z_buf, sp_buf, a_buf, u, h):
        fresh = qi_tab[n] == kj_tab[n]
        cs = _dot(sp_buf[u, h], ntri_ref[...])
        c = jnp.where(fresh, 0.0, c_ref[h])
        la = z_buf[u, h] + cs + jnp.concatenate([c] * (tk // LANES), axis=1)
        a_buf[u, h] = jnp.exp2(la).astype(BF16)
        c_ref[h] = c + jnp.broadcast_to(cs[:, 0:1], (tq, LANES))

    def values(n, a_buf, u, h):
        qi = qi_tab[n]
        kj = kj_tab[n]
        qs = pl.multiple_of(qi * tq, tq)
        ks = pl.multiple_of(kj * tk, tk)
        acc = jnp.where(qi == kj, 0.0, acc_ref[h]) + _dot(a_buf[u, h], v_ref[pl.ds(ks, tk), :])
        acc_ref[h] = acc
        lanes = slice(h * B_HDIM, (h + 1) * B_HDIM)
        y_ref[pl.ds(qs, tq), lanes] = acc[:, lanes].astype(BF16)

    def step(tau, par):
        static = isinstance(tau, int)
        ok = lambda g: not static or 0 <= g < n_groups
        for u in range(ATT_GROUP):
            for h in heads:
                if ok(tau):
                    logits(tau * ATT_GROUP + u, z_bufs[par], sp_bufs[par], u, h)
                if ok(tau - 1):
                    weights((tau - 1) * ATT_GROUP + u, z_bufs[1 - par], sp_bufs[1 - par],
                            a_bufs[1 - par], u, h)
                if ok(tau - 2):
                    values((tau - 2) * ATT_GROUP + u, a_bufs[par], u, h)

    for tau in range(2):
        step(tau, tau % 2)

    def body(i, _):
        tau = 2 + 2 * i
        step(tau, 0)
        step(tau + 1, 1)
        return 0

    n_body = max(n_groups - 2, 0) // 2
    lax.fori_loop(0, n_body, body, 0)
    for tau in range(2 + 2 * n_body, n_groups + 2):
        step(tau, tau % 2)


def _stick_breaking(qkv, gq, gk, bsz, seq):
    t = qkv.shape[0]
    n_pairs = B_HEADS // 2
    tq, tk = ATT_TQ, ATT_TK
    assert tq == tk and seq % tq == 0 and seq // tq >= 2
    nq = seq // tq
    tiles = [(qi, kj) for qi in range(nq) for kj in range(qi, -1, -1)]
    tiles += [(0, 0)] * (-len(tiles) % ATT_GROUP)
    qi_tab = jnp.asarray([t_[0] for t_ in tiles], jnp.int32)
    kj_tab = jnp.asarray([t_[1] for t_ in tiles], jnp.int32)
    seg = (jnp.arange(LANES)[:, None] // B_HDIM == jnp.arange(LANES)[None, :] // B_HDIM).astype(BF16)
    tri = (jnp.arange(tk)[:, None] >= jnp.arange(tk)[None, :]).astype(BF16)
    ntri = -tri
    strict = jnp.arange(tk)[None, :] < jnp.arange(tq)[:, None]
    bias = jnp.stack([jnp.zeros((tq, tk), F32), jnp.where(strict, 0.0, -1e30).astype(F32)])
    gq2 = jnp.tile(gq.reshape(1, B_HDIM), (1, 2)).astype(F32)
    gk2 = jnp.tile(gk.reshape(1, B_HDIM), (1, 2)).astype(F32)
    const = lambda shape: pl.BlockSpec(shape, lambda *_: (0,) * len(shape),
                                       pipeline_mode=pl.Buffered(1))
    grid_spec = pltpu.PrefetchScalarGridSpec(
        num_scalar_prefetch=2,
        grid=(bsz, n_pairs),
        in_specs=[
            pl.BlockSpec((seq, LANES), lambda b, p, *_: (b, p)),
            pl.BlockSpec((seq, LANES), lambda b, p, *_: (b, n_pairs + p)),
            pl.BlockSpec((seq, LANES), lambda b, p, *_: (b, 2 * n_pairs + p)),
            const((1, LANES)),
            const((1, LANES)),
            const((LANES, LANES)),
            const((tk, tk)),
            const((2, tq, tk)),
        ],
        out_specs=pl.BlockSpec((seq, LANES), lambda b, p, *_: (b, p)),
        scratch_shapes=[
            pltpu.VMEM((seq, LANES), BF16),
            pltpu.VMEM((seq, LANES), BF16),
            pltpu.VMEM((seq // tk, LANES, tk), BF16),
            pltpu.VMEM((ATT_GROUP, 2, tq, tk), F32),
            pltpu.VMEM((ATT_GROUP, 2, tq, tk), F32),
            pltpu.VMEM((ATT_GROUP, 2, tq, tk), BF16),
            pltpu.VMEM((ATT_GROUP, 2, tq, tk), BF16),
            pltpu.VMEM((ATT_GROUP, 2, tq, tk), BF16),
            pltpu.VMEM((ATT_GROUP, 2, tq, tk), BF16),
            pltpu.VMEM((2, tq, LANES), F32),
            pltpu.VMEM((2, tq, LANES), F32),
        ],
    )
    return pl.pallas_call(
        _sb_kernel,
        grid_spec=grid_spec,
        out_shape=jax.ShapeDtypeStruct((t, B_HEADS * B_HDIM), BF16),
        compiler_params=pltpu.CompilerParams(
            dimension_semantics=("arbitrary", "arbitrary"), vmem_limit_bytes=VMEM_LIMIT),
        name="stick_breaking",
    )(qi_tab, kj_tab, qkv, qkv, qkv, gq2, gk2, seg, ntri, bias)


def _odd_in_kernel(x_ref, g_ref, w_ref, wg_ref, p_ref, gates_ref):
    h = _rms_rows(x_ref[...], g_ref[...]).astype(BF16)
    p_ref[...] = _dot(h, w_ref[...]).astype(BF16)
    gates_ref[...] = _dot(h, wg_ref[...])


def _odd_in(x2, g, w_main, w_gates):
    t, d = x2.shape
    n_main = w_main.shape[1]
    tm = ROW_TILE
    return pl.pallas_call(
        _odd_in_kernel,
        grid=(t // tm,),
        in_specs=[
            pl.BlockSpec((tm, d), lambda i: (i, 0)),
            _const_spec((1, d)),
            _const_spec((d, n_main)),
            _const_spec((d, LANES)),
        ],
        out_specs=[
            pl.BlockSpec((tm, n_main), lambda i: (i, 0)),
            pl.BlockSpec((tm, LANES), lambda i: (i, 0)),
        ],
        out_shape=[
            jax.ShapeDtypeStruct((t, n_main), BF16),
            jax.ShapeDtypeStruct((t, LANES), F32),
        ],
        compiler_params=pltpu.CompilerParams(
            dimension_semantics=("arbitrary",), vmem_limit_bytes=VMEM_LIMIT),
        name="odd_in_proj",
    )(x2, g, w_main, w_gates)


def _mlstm_kernel(qk_ref, v_ref, og_ref, gates_ref, cw_ref, cb_ref, gb_ref, ong_ref,
                  tril3_ref, shift_ref, y_ref, prev_ref, c_ref, m_ref):
    n_qk = C_HEADS * C_QK
    n_chunks = qk_ref.shape[0] // CHUNK
    neg = -1e30
    hs = list(range(C_HEADS))

    @pl.when(pl.program_id(1) == 0)
    def _():
        prev_ref[...] = jnp.zeros_like(prev_ref)
        c_ref[...] = jnp.zeros_like(c_ref)
        m_ref[...] = jnp.zeros_like(m_ref)

    def conv_silu(ci):
        if ci == 0:
            ext = jnp.concatenate([prev_ref[...], qk_ref[:CHUNK, :]], axis=0)
        else:
            ext = qk_ref[ci * CHUNK - CONV_PAD:(ci + 1) * CHUNK, :]
        taps = _dot(shift_ref[...], ext)
        conv = cb_ref[...]
        for j in range(CONV_W):
            conv = conv + cw_ref[j:j + 1, :] * taps[j * CHUNK:(j + 1) * CHUNK, :]
        return conv * jax.nn.sigmoid(conv)

    col = lax.broadcasted_iota(jnp.int32, (CHUNK, LANES), 1)
    rr = lax.broadcasted_iota(jnp.int32, (CHUNK, CHUNK), 0)
    cc = lax.broadcasted_iota(jnp.int32, (CHUNK, CHUNK), 1)
    tril = cc <= rr
    ones_col = (col == 0).astype(BF16)
    c_st = [c_ref[h] for h in hs]
    m_st = [m_ref[:, h:h + 1] for h in hs]
    qk_act = conv_silu(0)

    for ci in range(n_chunks):
        rows = slice(ci * CHUNK, (ci + 1) * CHUNK)
        gpre = gates_ref[rows, :] + gb_ref[...]
        logf = jnp.minimum(gpre, 0.0) - jnp.log(1.0 + jnp.exp(-jnp.abs(gpre)))
        gx = jnp.where(col < C_HEADS, gpre, jnp.where(col < 2 * C_HEADS, logf, 0.0))
        bcum = _dot(tril3_ref[...], jnp.concatenate(_split_bf16(gx, 3), axis=0))
        gx_t = gx.T
        bcum_t = bcum.T

        q = [qk_act[:, h * C_QK:(h + 1) * C_QK] for h in hs]
        kf = [qk_act[:, n_qk + h * C_QK:n_qk + (h + 1) * C_QK] * (C_QK ** -0.5) for h in hs]
        qb = [q[h].astype(BF16) for h in hs]
        kb = [kf[h].astype(BF16) for h in hs]
        vaug = [jnp.concatenate([v_ref[rows, h * C_V:(h + 1) * C_V], ones_col], axis=1)
                for h in hs]
        b_col = [bcum[:, C_HEADS + h:C_HEADS + h + 1] for h in hs]
        b_row = [bcum_t[C_HEADS + h:C_HEADS + h + 1, :] for h in hs]
        i_col = [gx[:, h:h + 1] for h in hs]
        i_row = [gx_t[h:h + 1, :] for h in hs]
        b_last = [bcum[CHUNK - 1:CHUNK, C_HEADS + h:C_HEADS + h + 1] for h in hs]
        qk = [_dot_nt(qb[h], kb[h]) for h in hs]
        umat = [jnp.where(tril, i_row[h] - b_row[h], neg) for h in hs]
        g = [b_col[h] + m_st[h] for h in hs]
        m_t = [jnp.maximum(g[h], b_col[h] + jnp.max(umat[h], axis=-1, keepdims=True)) for h in hs]
        wts = [jnp.exp(umat[h] + (b_col[h] - m_t[h])) for h in hs]
        inter = [jnp.exp(g[h] - m_t[h]) for h in hs]
        lhs = [jnp.concatenate([(qk[h] * wts[h]).astype(BF16), (q[h] * inter[h]).astype(BF16)],
                               axis=1) for h in hs]
        rhs = [jnp.concatenate([vaug[h], c_st[h].astype(BF16)], axis=0) for h in hs]
        nd = [_dot(lhs[h], rhs[h]) for h in hs]

        if ci + 1 < n_chunks:
            qk_act = conv_silu(ci + 1)

        wk = [b_last[h] - b_col[h] + i_col[h] for h in hs]
        m_new = [jnp.maximum(b_last[h] + m_st[h], jnp.max(wk[h], axis=0, keepdims=True))
                 for h in hs]
        kw = [(kf[h] * jnp.exp(wk[h] - m_new[h])).astype(BF16) for h in hs]
        kv = [_dot_tn(kw[h], vaug[h]) for h in hs]
        c_st = [jnp.exp(b_last[h] + m_st[h] - m_new[h]) * c_st[h] + kv[h] for h in hs]
        m_st = m_new

        hh = [nd[h][:, :C_V] / jnp.maximum(jnp.abs(nd[h][:, C_V:C_V + 1]), jnp.exp(-m_t[h]))
              for h in hs]
        ms = [jnp.mean(hh[h] * hh[h], axis=-1, keepdims=True) for h in hs]
        for h in hs:
            hn = hh[h] * lax.rsqrt(ms[h] + EPS) * ong_ref[:, h * C_V:(h + 1) * C_V]
            og = og_ref[rows, h * C_V:(h + 1) * C_V].astype(F32)
            y_ref[rows, h * C_V:(h + 1) * C_V] = (hn * jax.nn.sigmoid(og)).astype(BF16)

    prev_ref[...] = qk_ref[n_chunks * CHUNK - CONV_PAD:, :]
    for h in hs:
        c_ref[h] = c_st[h]
        m_ref[:, h:h + 1] = m_st[h]


def _mlstm(p_main, gates, conv_w, conv_b, gate_b, on_g, bsz, seq):
    t = p_main.shape[0]
    n_qk2 = 2 * C_HEADS * C_QK
    n_v = C_HEADS * C_V
    assert n_qk2 == n_v
    rows = MLSTM_CHUNKS * CHUNK
    assert seq % rows == 0
    ns = seq // rows
    tril = (jnp.arange(CHUNK)[:, None] >= jnp.arange(CHUNK)[None, :]).astype(BF16)
    tril3 = jnp.concatenate([tril, tril, tril], axis=1)
    t_idx = jnp.arange(CONV_W * CHUNK)
    src = CONV_PAD + t_idx % CHUNK - (CONV_W - 1) + t_idx // CHUNK
    shift = (src[:, None] == jnp.arange(CONV_PAD + CHUNK)[None, :]).astype(BF16)
    return pl.pallas_call(
        _mlstm_kernel,
        grid=(bsz, ns),
        in_specs=[
            pl.BlockSpec((rows, n_qk2), lambda b, c: (b * ns + c, 0)),
            pl.BlockSpec((rows, n_v), lambda b, c: (b * ns + c, 1)),
            pl.BlockSpec((rows, n_v), lambda b, c: (b * ns + c, 2)),
            pl.BlockSpec((rows, LANES), lambda b, c: (b * ns + c, 0)),
            _const_spec((CONV_W, n_qk2)),
            _const_spec((1, n_qk2)),
            _const_spec((1, LANES)),
            _const_spec((1, n_v)),
            _const_spec((CHUNK, 3 * CHUNK)),
            _const_spec((CONV_W * CHUNK, CONV_PAD + CHUNK)),
        ],
        out_specs=pl.BlockSpec((rows, n_v), lambda b, c: (b * ns + c, 0)),
        out_shape=jax.ShapeDtypeStruct((t, n_v), BF16),
        scratch_shapes=[
            pltpu.VMEM((CONV_PAD, n_qk2), BF16),
            pltpu.VMEM((C_HEADS, C_QK, C_AUG), F32),
            pltpu.VMEM((1, LANES), F32),
        ],
        compiler_params=pltpu.CompilerParams(
            dimension_semantics=("arbitrary", "arbitrary"), vmem_limit_bytes=VMEM_LIMIT),
        name="mlstm",
    )(p_main, p_main, p_main, gates, conv_w, conv_b, gate_b, on_g, tril3, shift)


def _out_mlp_kernel(n_y, *refs):
    x_ref = refs[0]
    y_refs = refs[1:1 + n_y]
    wo_ref, g_ref, w1_ref, w2_ref, o_ref = refs[1 + n_y:]
    d = x_ref.shape[1]
    y = jnp.concatenate([r[...] for r in y_refs], axis=1) if n_y > 1 else y_refs[0][...]
    x1 = x_ref[...] + _dot(y, wo_ref[...])
    h = _rms_rows(x1, g_ref[...]).astype(BF16)
    acc = x1
    for c0 in range(0, w1_ref.shape[1], d):
        a = jnp.maximum(_dot(h, w1_ref[:, c0:c0 + d]), 0.0)
        acc = acc + _dot((a * a).astype(BF16), w2_ref[c0:c0 + d, :])
    o_ref[...] = acc


def _out_mlp(x2, ys, w_out, g, w1, w2):
    t, d = x2.shape
    d_ff = w1.shape[1]
    tm = ROW_TILE
    in_specs = [pl.BlockSpec((tm, d), lambda i: (i, 0))]
    in_specs += [pl.BlockSpec((tm, y.shape[1]), lambda i: (i, 0)) for y in ys]
    in_specs += [_const_spec((d, d)), _const_spec((1, d)), _const_spec((d, d_ff)),
                 _const_spec((d_ff, d))]
    return pl.pallas_call(
        functools.partial(_out_mlp_kernel, len(ys)),
        grid=(t // tm,),
        in_specs=in_specs,
        out_specs=pl.BlockSpec((tm, d), lambda i: (i, 0)),
        out_shape=jax.ShapeDtypeStruct((t, d), F32),
        compiler_params=pltpu.CompilerParams(
            dimension_semantics=("arbitrary",), vmem_limit_bytes=VMEM_LIMIT),
        name="out_proj_mlp",
    )(x2, *ys, w_out, g, w1, w2)


def kernel(x, mix_norm_g, mlp_norm_g, mlp_w1, mlp_w2, ev_w_in, ev_w_out, sg_ln_g, sg_ln_b, sg_w, sg_b, sb_q_norm_g, sb_k_norm_g, od_w_in, od_conv_w, od_conv_b, od_i_b, od_f_b, od_out_norm_g, od_w_out):
    bsz, seq, d = x.shape
    depth = mix_norm_g.shape[0]
    a_width = A_GROUPS * A_GDIM
    n_main = 2 * C_HEADS * C_QK + 2 * C_HEADS * C_V
    x2 = x.reshape(bsz * seq, d)
    row = lambda v: v.reshape(1, -1).astype(F32)
    for layer in range(depth):
        j = layer // 2
        g_mix = row(mix_norm_g[layer])
        if layer % 2 == 0:
            sgw_pairs = jnp.concatenate([sg_w[j, 0::2], sg_w[j, 1::2]], axis=-1)
            sgb_full = jnp.repeat(sg_b[j].T, A_GDIM, axis=1)
            ya, qkv = _even_in(x2, g_mix, ev_w_in[j].astype(BF16), row(sg_ln_g[j]),
                               row(sg_ln_b[j]), sgw_pairs, sgb_full)
            yb = _stick_breaking(qkv, sb_q_norm_g[j], sb_k_norm_g[j], bsz, seq)
            ys = (ya, yb)
            w_out = ev_w_out[j]
        else:
            w_in = od_w_in[j]
            w_gates = jnp.pad(w_in[:, n_main:], ((0, 0), (0, LANES - 2 * C_HEADS)))
            p_main, gates = _odd_in(x2, g_mix, w_in[:, :n_main].astype(BF16),
                                    w_gates.astype(BF16))
            gate_b = jnp.pad(jnp.concatenate([od_i_b[j], od_f_b[j]]),
                             (0, LANES - 2 * C_HEADS)).reshape(1, LANES)
            y = _mlstm(p_main, gates, od_conv_w[j], row(od_conv_b[j]), gate_b,
                       row(od_out_norm_g[j]), bsz, seq)
            ys = (y,)
            w_out = od_w_out[j]
        x2 = _out_mlp(x2, ys, w_out.astype(BF16), row(mlp_norm_g[layer]),
                      mlp_w1[layer].astype(BF16), mlp_w2[layer].astype(BF16))
    return x2.reshape(bsz, seq, d)
```
